```python
import jax
import jax.numpy as jnp
from jax import lax
import numpy as np

D_MODEL = 1024
BATCH = 8
SEQ = 2048
DEPTH = 4
DEC_BATCH = 32
DEC_SEQ = 1
PAST_LEN = 8192
PAGE_SIZE = 128

N_MIXERS = 3
N_NSA = (DEPTH + 2) // 3
N_RWKV = (DEPTH + 1) // 3
N_CONV = DEPTH // 3

HEAD_DIM = 64
N_HEADS = D_MODEL // HEAD_DIM
N_KV_HEADS = 4
GROUP = N_HEADS // N_KV_HEADS
KV_W = N_KV_HEADS * HEAD_DIM
CMP_LEN = 32
CMP_STRIDE = 16
CMP_RATIO = CMP_LEN // CMP_STRIDE
CMP_HIDDEN = 2 * HEAD_DIM
SEL_BLOCK = 64
N_SEL = 16
WINDOW = 512
Q_BLOCK = 32
ROPE_THETA = 10000.0
NSA_IN = N_HEADS * HEAD_DIM + 6 * KV_W + 3 * N_HEADS

RWKV_HEAD = 64
RWKV_HEADS = D_MODEL // RWKV_HEAD
DECAY_LORA = 64
AAA_LORA = 64
GATE_LORA = 128
GN_EPS = RWKV_HEAD * 1e-5

CONV_W = 31
D_FF = -(-8 * D_MODEL // (3 * 256)) * 256
RMS_EPS = 1e-6
LN_EPS = 1e-5

kernel_name = 'hybrid_nsa_rwkv7_conformer_step'


def rmsnorm(x, g):
    xf = x.astype(jnp.float32)
    y = xf * lax.rsqrt(jnp.mean(xf * xf, axis=-1, keepdims=True) + RMS_EPS)
    return (y * g.astype(jnp.float32)).astype(x.dtype)


def layernorm_f32(x, g, b, eps):
    xf = x.astype(jnp.float32)
    mu = jnp.mean(xf, axis=-1, keepdims=True)
    var = jnp.mean(jnp.square(xf - mu), axis=-1, keepdims=True)
    return (xf - mu) * lax.rsqrt(var + eps) * g.astype(jnp.float32) + b.astype(jnp.float32)


def masked_softmax(s, mask):
    s = jnp.where(mask, s, -jnp.inf)
    m = jnp.max(s, axis=-1, keepdims=True)
    m = jnp.where(jnp.isfinite(m), m, 0.0)
    e = jnp.where(mask, jnp.exp(s - m), 0.0)
    d = jnp.sum(e, axis=-1, keepdims=True)
    return e / jnp.where(d > 0.0, d, 1.0)


def rope(x, pos):
    half = HEAD_DIM // 2
    inv = ROPE_THETA ** (-jnp.arange(half, dtype=jnp.float32) / half)
    ang = pos.astype(jnp.float32)[:, None] * inv
    cos = jnp.cos(ang)[None, :, None, :]
    sin = jnp.sin(ang)[None, :, None, :]
    xf = x.astype(jnp.float32)
    x1, x2 = xf[..., :half], xf[..., half:]
    return jnp.concatenate([x1 * cos - x2 * sin, x2 * cos + x1 * sin], axis=-1).astype(x.dtype)


def swiglu(h, w_gu, w_down):
    gu = h @ w_gu
    return (jax.nn.silu(gu[..., :D_FF]) * gu[..., D_FF:]) @ w_down


def cmp_to_sel(n_cmp, n_slc):
    c0 = jnp.arange(n_cmp)[:, None] * CMP_STRIDE
    j0 = jnp.arange(n_slc)[None, :] * SEL_BLOCK
    return ((c0 < j0 + SEL_BLOCK) & (c0 + CMP_LEN > j0)).astype(jnp.float32)


def nsa_project(h, pos, w_in, q_norm, k_norm):
    B, T, _ = h.shape
    qd = N_HEADS * HEAD_DIM
    proj = h @ w_in
    q = rmsnorm(proj[..., :qd].reshape(B, T, N_HEADS, HEAD_DIM), q_norm)
    kv = proj[..., qd:qd + 6 * KV_W].reshape(B, T, 6, N_KV_HEADS, HEAD_DIM)
    gates = jax.nn.sigmoid(proj[..., qd + 6 * KV_W:].astype(jnp.float32)).reshape(B, T, N_HEADS, 3)
    k_sel = rope(rmsnorm(kv[:, :, 2], k_norm[1]), pos)
    k_win = rope(rmsnorm(kv[:, :, 4], k_norm[2]), pos)
    rows = jnp.stack([kv[:, :, 0], kv[:, :, 1], k_sel, kv[:, :, 3]], axis=2)
    wrows = jnp.stack([k_win, kv[:, :, 5]], axis=2)
    return q, rope(q, pos), gates, rows, wrows


def compress(rows, pe, w1, w2, k_gain):
    B, L = rows.shape[:2]
    n_cmp = (L - CMP_LEN) // CMP_STRIDE + 1
    n_seg = -(-L // CMP_STRIDE)
    rows = jnp.pad(rows, ((0, 0), (0, n_seg * CMP_STRIDE - L), (0, 0), (0, 0), (0, 0)))
    seg = rows.reshape(B, n_seg, CMP_STRIDE, 2, N_KV_HEADS, HEAD_DIM)
    w1r = w1.reshape(2, CMP_RATIO, CMP_STRIDE, HEAD_DIM, CMP_HIDDEN)
    part = jnp.einsum('bnlsgd,srldf->bnrsgf', seg, w1r)
    hid = jnp.einsum('sld,sldf->sf', pe, w1)[:, None, :]
    for r in range(CMP_RATIO):
        hid = hid + part[:, r:r + n_cmp, r]
    out = jnp.einsum('bcsgf,sfd->bcsgd', jax.nn.silu(hid), w2)
    return rmsnorm(out[:, :, 0], k_gain), out[:, :, 1]


def nsa_attend(q_nope, q_rope, gates, pos_q, kc, vc, gather_sel, n_slc, kw, vw, pos_w):
    B, Tq = q_nope.shape[:2]
    f32 = jnp.float32
    scale = HEAD_DIM ** -0.5
    grp = lambda a: a.reshape(B, Tq, N_KV_HEADS, GROUP, HEAD_DIM)
    t = pos_q[:, None]
    n_cmp = kc.shape[1]
    cmp_end = jnp.arange(n_cmp) * CMP_STRIDE + (CMP_LEN - 1)
    s = jnp.einsum('btgnd,bcgd->btgnc', grp(q_nope), kc, preferred_element_type=f32) * scale
    p_cmp = masked_softmax(s, (cmp_end[None, :] <= t)[None, :, None, None, :])
    o_cmp = jnp.einsum('btgnc,bcgd->btgnd', p_cmp.astype(vc.dtype), vc, preferred_element_type=f32)
    score = jnp.einsum('btgnc,cj->btgj', p_cmp, cmp_to_sel(n_cmp, n_slc))
    blk = jnp.arange(n_slc)[None, :]
    cur = t // SEL_BLOCK
    adm = blk * SEL_BLOCK <= t
    forced = adm & ((blk == 0) | (blk == cur) | (blk == cur - 1))
    score = jnp.where(forced[None, :, None, :], jnp.inf, jnp.where(adm[None, :, None, :], score, -jnp.inf))
    top, idx = lax.top_k(score, min(N_SEL, n_slc))
    n_top = idx.shape[-1]
    kv = gather_sel(idx)
    tok = idx[..., None] * SEL_BLOCK + jnp.arange(SEL_BLOCK)
    m_sel = (top > -jnp.inf)[..., None] & (tok <= pos_q[None, :, None, None, None])
    m_sel = m_sel.reshape(B, Tq, N_KV_HEADS, 1, n_top * SEL_BLOCK)
    k_s = kv[..., 0, :].reshape(B, Tq, N_KV_HEADS, n_top * SEL_BLOCK, HEAD_DIM)
    v_s = kv[..., 1, :].reshape(B, Tq, N_KV_HEADS, n_top * SEL_BLOCK, HEAD_DIM)
    s = jnp.einsum('btgnd,btgxd->btgnx', grp(q_rope), k_s, preferred_element_type=f32) * scale
    p = masked_softmax(s, m_sel)
    o_sel = jnp.einsum('btgnx,btgxd->btgnd', p.astype(v_s.dtype), v_s, preferred_element_type=f32)
    pw = pos_w[None, :]
    m_w = (pw <= t) & (pw > t - WINDOW) & (pw >= 0)
    s = jnp.einsum('btgnd,bsgd->btgns', grp(q_rope), kw, preferred_element_type=f32) * scale
    p = masked_softmax(s, m_w[None, :, None, None, :])
    o_win = jnp.einsum('btgns,bsgd->btgnd', p.astype(vw.dtype), vw, preferred_element_type=f32)
    g = gates.reshape(B, Tq, N_KV_HEADS, GROUP, 3)
    o = g[..., 0:1] * o_cmp + g[..., 1:2] * o_sel + g[..., 2:3] * o_win
    return o.reshape(B, Tq, N_HEADS, HEAD_DIM).astype(q_nope.dtype)


def nsa_prompt(h, w_in, q_norm, k_norm, cmp_pe, cmp_w1, cmp_w2, w_out):
    B, T, _ = h.shape
    q, q_rope, gates, rows, wrows = nsa_project(h, jnp.arange(T), w_in, q_norm, k_norm)
    kc, vc = compress(rows[:, :, :2], cmp_pe, cmp_w1, cmp_w2, k_norm[0])
    n_slc = T // SEL_BLOCK
    blocks = rows[:, :, 2:].reshape(B, n_slc, SEL_BLOCK, 2, N_KV_HEADS, HEAD_DIM).transpose(0, 1, 4, 2, 3, 5)
    b_ix = jnp.arange(B)[:, None, None, None]
    g_ix = jnp.arange(N_KV_HEADS)[None, None, :, None]

    def gather_sel(idx):
        return blocks[b_ix, idx, g_ix]

    wpad = jnp.pad(wrows, ((0, 0), (WINDOW, 0), (0, 0), (0, 0), (0, 0)))

    def one_block(c):
        s0 = c * Q_BLOCK
        sl = lambda a: lax.dynamic_slice_in_dim(a, s0, Q_BLOCK, axis=1)
        wk = lax.dynamic_slice_in_dim(wpad, s0, WINDOW + Q_BLOCK, axis=1)
        return nsa_attend(sl(q), sl(q_rope), sl(gates), s0 + jnp.arange(Q_BLOCK), kc, vc, gather_sel, n_slc,
                          wk[:, :, 0], wk[:, :, 1], s0 - WINDOW + jnp.arange(WINDOW + Q_BLOCK))

    o = lax.map(one_block, jnp.arange(T // Q_BLOCK))
    o = o.transpose(1, 0, 2, 3, 4).reshape(B, T, N_HEADS * HEAD_DIM)
    return o @ w_out, rows, wrows[:, T - min(WINDOW, T):]


def nsa_sample(h, cache, j, win_buf, page_table, w_in, q_norm, k_norm, cmp_pe, cmp_w1, cmp_w2, w_out):
    B, T, _ = h.shape
    n_pages = PAST_LEN // PAGE_SIZE
    past_len = n_pages * PAGE_SIZE
    pos = past_len + jnp.arange(T)
    q, q_rope, gates, rows, wrows = nsa_project(h, pos, w_in, q_norm, k_norm)
    past_cmp = cache[j, page_table, :, :2].reshape(B, past_len, 2, N_KV_HEADS, HEAD_DIM)
    kc, vc = compress(jnp.concatenate([past_cmp, rows[:, :, :2]], axis=1), cmp_pe, cmp_w1, cmp_w2, k_norm[0])
    n_slc = -(-(past_len + T) // SEL_BLOCK)
    n_past_blk = past_len // SEL_BLOCK
    n_new_blk = n_slc - n_past_blk
    per_page = PAGE_SIZE // SEL_BLOCK
    new = jnp.pad(rows[:, :, 2:], ((0, 0), (0, n_new_blk * SEL_BLOCK - T), (0, 0), (0, 0), (0, 0)))
    new_blocks = new.reshape(B, n_new_blk, SEL_BLOCK, 2, N_KV_HEADS, HEAD_DIM).transpose(0, 1, 4, 2, 3, 5)
    b_ix = jnp.arange(B)[:, None, None, None]
    g_ix = jnp.arange(N_KV_HEADS)[None, None, :, None]
    g_ix6 = jnp.arange(N_KV_HEADS)[None, None, :, None, None, None]
    slot_ix = jnp.array([2, 3], dtype=jnp.int32)

    def gather_sel(idx):
        is_past = idx < n_past_blk
        ip = jnp.minimum(idx, n_past_blk - 1)
        page = page_table[b_ix, ip // per_page]
        off = (ip % per_page)[..., None] * SEL_BLOCK + jnp.arange(SEL_BLOCK)
        kv_past = cache[j, page[..., None, None], off[..., None], slot_ix, g_ix6]
        kv_new = new_blocks[b_ix, jnp.clip(idx - n_past_blk, 0, n_new_blk - 1), g_ix]
        return jnp.where(is_past[..., None, None, None], kv_past, kv_new)

    wk = jnp.concatenate([win_buf.astype(wrows.dtype), wrows], axis=1)
    wb = win_buf.shape[1]
    o = nsa_attend(q, q_rope, gates, pos, kc, vc, gather_sel, n_slc, wk[:, :, 0], wk[:, :, 1],
                   past_len - wb + jnp.arange(wb + T))
    return o.reshape(B, T, N_HEADS * HEAD_DIM) @ w_out, rows, wk[:, T:]


def rwkv7_mix(h, shift_prev, wkv0, mix, w_rkv, w0, w1, w2, a0, a1, a2, g1, g2, k_k, k_a, r_k, ln_w, ln_b, w_out):
    B, T, D = h.shape
    f32 = jnp.float32
    heads = lambda z: z.astype(f32).reshape(B, T, RWKV_HEADS, RWKV_HEAD)
    prev = jnp.concatenate([shift_prev[:, None].astype(h.dtype), h[:, :-1]], axis=1)
    xm = h[:, :, None] + (prev - h)[:, :, None] * mix
    rkv = jnp.einsum('btsd,sde->btse', xm[:, :, :3], w_rkv)
    r, k, v = heads(rkv[:, :, 0]), rkv[:, :, 1].astype(f32), heads(rkv[:, :, 2])
    w_log = -jax.nn.softplus(-(w0 + jnp.tanh(xm[:, :, 3] @ w1) @ w2).astype(f32)) - 0.5
    decay = heads(jnp.exp(-jnp.exp(w_log)))
    a = jax.nn.sigmoid((a0 + (xm[:, :, 4] @ a1) @ a2).astype(f32))
    g = (jax.nn.sigmoid(xm[:, :, 5] @ g1) @ g2).astype(f32)
    kk = heads(k * k_k)
    kk = kk / jnp.maximum(jnp.sqrt(jnp.sum(kk * kk, axis=-1, keepdims=True)), 1e-12)
    k = heads(k * (1.0 + (a - 1.0) * k_a))
    a = heads(a)

    def step(S, inp):
        r_t, w_t, k_t, v_t, kk_t, a_t = inp
        sa = jnp.einsum('bhij,bhj->bhi', S, -kk_t)
        S = S * w_t[:, :, None, :] + sa[..., None] * (kk_t * a_t)[:, :, None, :] + v_t[..., None] * k_t[:, :, None, :]
        return S, jnp.einsum('bhij,bhj->bhi', S, r_t)

    tm = lambda z: jnp.moveaxis(z, 1, 0)
    S_T, y = lax.scan(step, wkv0.astype(f32), (tm(r), tm(decay), tm(k), tm(v), tm(kk), tm(a)))
    y = jnp.moveaxis(y, 0, 1)
    y = layernorm_f32(y, ln_w.reshape(RWKV_HEADS, RWKV_HEAD), ln_b.reshape(RWKV_HEADS, RWKV_HEAD), GN_EPS)
    y = y + jnp.sum(r * k * r_k, axis=-1, keepdims=True) * v
    out = (y.reshape(B, T, D) * g).astype(h.dtype) @ w_out
    return out, h[:, -1], S_T.astype(wkv0.dtype)


def conv_module(h, buf, w_pw1, b_pw1, w_dw, b_dw, ln_w, ln_b, w_pw2, b_pw2):
    u = h @ w_pw1 + b_pw1
    glu = u[..., :D_MODEL] * jax.nn.sigmoid(u[..., D_MODEL:])
    ext = jnp.concatenate([buf.astype(glu.dtype), glu], axis=1)
    y = lax.conv_general_dilated(ext, w_dw[:, None, :].astype(ext.dtype), window_strides=(1,), padding='VALID',
                                 dimension_numbers=('NWC', 'WIO', 'NWC'), feature_group_count=D_MODEL) + b_dw
    y = jax.nn.silu(layernorm_f32(y, ln_w, ln_b, LN_EPS)).astype(h.dtype)
    return y @ w_pw2 + b_pw2, ext[:, -(CONV_W - 1):]


def setup_inputs(seed: int = 0) -> dict:
    key = jax.random.key(seed)
    keys = iter(jax.random.split(key, 48))
    f32 = jnp.float32

    def nrm(shape, scale):
        return jax.random.normal(next(keys), shape, f32) * scale

    def gain(shape):
        return 1.0 + nrm(shape, 0.02)

    n_pages = PAST_LEN // PAGE_SIZE
    n_used = DEC_BATCH * n_pages
    n_pool = n_used + -(-n_used // 4)
    win_buf = min(WINDOW, PAST_LEN)
    page_table = jax.random.permutation(next(keys), n_pool)[:n_used].reshape(DEC_BATCH, n_pages).astype(jnp.int32)
    return {
        'x_prompt': nrm((BATCH, SEQ, D_MODEL), 1.0),
        'x_sample': nrm((DEC_BATCH, DEC_SEQ, D_MODEL), 1.0),
        'cache_nsa': nrm((N_NSA, n_pool, PAGE_SIZE, 4, N_KV_HEADS, HEAD_DIM), 1.0),
        'cache_nsa_win': nrm((N_NSA, DEC_BATCH, win_buf, 2, N_KV_HEADS, HEAD_DIM), 1.0),
        'state_rwkv_wkv': nrm((N_RWKV, DEC_BATCH, RWKV_HEADS, RWKV_HEAD, RWKV_HEAD), 0.3),
        'state_rwkv_shift': nrm((N_RWKV, DEC_BATCH, D_MODEL), 1.0),
        'state_conv': nrm((N_CONV, DEC_BATCH, CONV_W - 1, D_MODEL), 0.5),
        'page_table': page_table,
        'norms': gain((DEPTH, 2, D_MODEL)),
        'ffn_w_gu': nrm((DEPTH, D_MODEL, 2 * D_FF), D_MODEL ** -0.5),
        'ffn_w_down': nrm((DEPTH, D_FF, D_MODEL), D_FF ** -0.5),
        'nsa_w_in': nrm((N_NSA, D_MODEL, NSA_IN), D_MODEL ** -0.5),
        'nsa_q_norm': gain((N_NSA, HEAD_DIM)),
        'nsa_k_norm': gain((N_NSA, 3, HEAD_DIM)),
        'nsa_cmp_pe': nrm((N_NSA, 2, CMP_LEN, HEAD_DIM), 0.5),
        'nsa_cmp_w1': nrm((N_NSA, 2, CMP_LEN, HEAD_DIM, CMP_HIDDEN), (CMP_LEN * HEAD_DIM) ** -0.5),
        'nsa_cmp_w2': nrm((N_NSA, 2, CMP_HIDDEN, HEAD_DIM), CMP_HIDDEN ** -0.5),
        'nsa_w_out': nrm((N_NSA, N_HEADS * HEAD_DIM, D_MODEL), (N_HEADS * HEAD_DIM) ** -0.5),
        'rwkv_mix': jax.random.uniform(next(keys), (N_RWKV, 6, D_MODEL), f32),
        'rwkv_w_rkv': nrm((N_RWKV, 3, D_MODEL, D_MODEL), D_MODEL ** -0.5),
        'rwkv_w0': nrm((N_RWKV, D_MODEL), 0.5),
        'rwkv_w1': nrm((N_RWKV, D_MODEL, DECAY_LORA), D_MODEL ** -0.5),
        'rwkv_w2': nrm((N_RWKV, DECAY_LORA, D_MODEL), 0.5 * DECAY_LORA ** -0.5),
        'rwkv_a0': nrm((N_RWKV, D_MODEL), 0.5),
        'rwkv_a1': nrm((N_RWKV, D_MODEL, AAA_LORA), D_MODEL ** -0.5),
        'rwkv_a2': nrm((N_RWKV, AAA_LORA, D_MODEL), 0.5 * AAA_LORA ** -0.5),
        'rwkv_g1': nrm((N_RWKV, D_MODEL, GATE_LORA), D_MODEL ** -0.5),
        'rwkv_g2': nrm((N_RWKV, GATE_LORA, D_MODEL), GATE_LORA ** -0.5),
        'rwkv_k_k': 0.85 + nrm((N_RWKV, D_MODEL), 0.05),
        'rwkv_k_a': 1.0 + nrm((N_RWKV, D_MODEL), 0.05),
        'rwkv_r_k': nrm((N_RWKV, RWKV_HEADS, RWKV_HEAD), 0.1),
        'rwkv_ln_w': gain((N_RWKV, D_MODEL)),
        'rwkv_ln_b': nrm((N_RWKV, D_MODEL), 0.02),
        'rwkv_w_out': nrm((N_RWKV, D_MODEL, D_MODEL), D_MODEL ** -0.5),
        'conv_w_pw1': nrm((N_CONV, D_MODEL, 2 * D_MODEL), D_MODEL ** -0.5),
        'conv_b_pw1': nrm((N_CONV, 2 * D_MODEL), 0.02),
        'conv_w_dw': nrm((N_CONV, CONV_W, D_MODEL), CONV_W ** -0.5),
        'conv_b_dw': nrm((N_CONV, D_MODEL), 0.02),
        'conv_ln_w': gain((N_CONV, D_MODEL)),
        'conv_ln_b': nrm((N_CONV, D_MODEL), 0.02),
        'conv_w_pw2': nrm((N_CONV, D_MODEL, D_MODEL), D_MODEL ** -0.5),
        'conv_b_pw2': nrm((N_CONV, D_MODEL), 0.02),
    }


def reference(x_prompt, x_sample, cache_nsa, cache_nsa_win, state_rwkv_wkv, state_rwkv_shift, state_conv,
              page_table, norms, ffn_w_gu, ffn_w_down,
              nsa_w_in, nsa_q_norm, nsa_k_norm, nsa_cmp_pe, nsa_cmp_w1, nsa_cmp_w2, nsa_w_out,
              rwkv_mix, rwkv_w_rkv, rwkv_w0, rwkv_w1, rwkv_w2, rwkv_a0, rwkv_a1, rwkv_a2, rwkv_g1, rwkv_g2,
              rwkv_k_k, rwkv_k_a, rwkv_r_k, rwkv_ln_w, rwkv_ln_b, rwkv_w_out,
              conv_w_pw1, conv_b_pw1, conv_w_dw, conv_b_dw, conv_ln_w, conv_ln_b, conv_w_pw2, conv_b_pw2):
    xp, xs = x_prompt, x_sample
    bp = xp.shape[0]
    rows_p, rows_s, win_p, win_s = [], [], [], []
    wkv_p, wkv_s, shift_p, shift_s, conv_p, conv_s = [], [], [], [], [], []
    for i in range(DEPTH):
        kind, j = i % N_MIXERS, i // N_MIXERS
        hp = rmsnorm(xp, norms[i, 0])
        hs = rmsnorm(xs, norms[i, 0])
        if kind == 0:
            w = (nsa_w_in[j], nsa_q_norm[j], nsa_k_norm[j], nsa_cmp_pe[j], nsa_cmp_w1[j], nsa_cmp_w2[j], nsa_w_out[j])
            mp, r_p, wn_p = nsa_prompt(hp, *w)
            ms, r_s, wn_s = nsa_sample(hs, cache_nsa, j, cache_nsa_win[j], page_table, *w)
            rows_p.append(r_p)
            rows_s.append(r_s)
            win_p.append(wn_p)
            win_s.append(wn_s)
        elif kind == 1:
            w = (rwkv_mix[j], rwkv_w_rkv[j], rwkv_w0[j], rwkv_w1[j], rwkv_w2[j], rwkv_a0[j], rwkv_a1[j], rwkv_a2[j],
                 rwkv_g1[j], rwkv_g2[j], rwkv_k_k[j], rwkv_k_a[j], rwkv_r_k[j], rwkv_ln_w[j], rwkv_ln_b[j], rwkv_w_out[j])
            mp, sh_p, st_p = rwkv7_mix(hp, jnp.zeros((bp, D_MODEL), hp.dtype),
                                       jnp.zeros((bp, RWKV_HEADS, RWKV_HEAD, RWKV_HEAD), jnp.float32), *w)
            ms, sh_s, st_s = rwkv7_mix(hs, state_rwkv_shift[j], state_rwkv_wkv[j], *w)
            wkv_p.append(st_p)
            wkv_s.append(st_s)
            shift_p.append(sh_p)
            shift_s.append(sh_s)
        else:
            w = (conv_w_pw1[j], conv_b_pw1[j], conv_w_dw[j], conv_b_dw[j], conv_ln_w[j], conv_ln_b[j],
                 conv_w_pw2[j], conv_b_pw2[j])
            mp, cb_p = conv_module(hp, jnp.zeros((bp, CONV_W - 1, D_MODEL), hp.dtype), *w)
            ms, cb_s = conv_module(hs, state_conv[j], *w)
            conv_p.append(cb_p)
            conv_s.append(cb_s)
        xp = xp + mp
        xs = xs + ms
        xp = xp + swiglu(rmsnorm(xp, norms[i, 1]), ffn_w_gu[i], ffn_w_down[i])
        xs = xs + swiglu(rmsnorm(xs, norms[i, 1]), ffn_w_gu[i], ffn_w_down[i])
    return (xp, xs, jnp.stack(rows_p), jnp.stack(rows_s), jnp.stack(win_p), jnp.stack(win_s),
            jnp.stack(wkv_p), jnp.stack(wkv_s), jnp.stack(shift_p), jnp.stack(shift_s),
            jnp.stack(conv_p), jnp.stack(conv_s))
```

```python
import functools

import jax
import jax.numpy as jnp
from jax import lax
from jax.experimental import pallas as pl
from jax.experimental.pallas import tpu as pltpu

F32 = jnp.float32
BF16 = jnp.bfloat16
HIGHEST = lax.Precision.HIGHEST

LANES = 128
HEAD_DIM = 64
N_KV_HEADS = 4
GROUP = 4
KV_W = N_KV_HEADS * HEAD_DIM
CMP_LEN = 32
CMP_STRIDE = 16
CMP_RATIO = CMP_LEN // CMP_STRIDE
CMP_HIDDEN = 2 * HEAD_DIM
SEL_BLOCK = 64
N_SEL = 16
WINDOW = 512
ROPE_THETA = 10000.0
PAGE_SIZE = 128
RWKV_HEAD = 64
GN_EPS = RWKV_HEAD * 1e-5
CONV_W = 31
RMS_EPS = 1e-6
LN_EPS = 1e-5
SCALE = HEAD_DIM ** -0.5
NEG = -1e30
RWKV_CHUNK = 16
VMEM_LIMIT = 56 * 1024 * 1024


def _cp(*sem):
    return pltpu.CompilerParams(dimension_semantics=sem, vmem_limit_bytes=VMEM_LIMIT)


def _rms(x, g):
    return x * lax.rsqrt(jnp.mean(x * x, axis=-1, keepdims=True) + RMS_EPS) * g


def _dot(a, b):
    return jnp.dot(a, b, preferred_element_type=F32)


def _dot_nt(a, b):
    return lax.dot_general(a, b, (((1,), (1,)), ((), ())), preferred_element_type=F32)


def _dot_tn(a, b):
    return lax.dot_general(a, b, (((0,), (0,)), ((), ())), preferred_element_type=F32)


def _dot_hp(a, b):
    return jnp.dot(a, b, preferred_element_type=F32, precision=HIGHEST)


def _split(x):
    hi = x.astype(BF16)
    return hi, (x - hi.astype(F32)).astype(BF16)


def _seg_sum64(v, bd, two_pass=False):
    outs = []
    for c in range(v.shape[-1] // 256):
        t = v[:, c * 256:(c + 1) * 256]
        if two_pass:
            hi, lo = _split(t)
            outs.append(_dot(hi, bd) + _dot(lo, bd))
        else:
            outs.append(_dot(t.astype(BF16), bd))
    return outs[0] if len(outs) == 1 else jnp.concatenate(outs, axis=1)


def _lane_iota(shape):
    return lax.broadcasted_iota(jnp.int32, shape, len(shape) - 1)


def _rope(xn, cosw, sinw):
    w = xn.shape[-1]
    first = (_lane_iota(xn.shape) % HEAD_DIM) < HEAD_DIM // 2
    sw = jnp.where(first, pltpu.roll(xn, w - HEAD_DIM // 2, 1), pltpu.roll(xn, HEAD_DIM // 2, 1))
    return xn * cosw + sw * sinw


def _dup64(x):
    outs = []
    for c in range(x.shape[-1] // LANES):
        t = x[:, c * LANES:(c + 1) * LANES]
        r = pltpu.roll(t, HEAD_DIM, 1)
        lo = _lane_iota(t.shape) < HEAD_DIM
        outs.append(jnp.where(lo, t, r))
        outs.append(jnp.where(lo, r, t))
    return jnp.concatenate(outs, axis=1)


def _row_tile(m, cap):
    t = min(m, cap)
    while m % t:
        t //= 2
    return t


def _ffn_kernel(x_ref, g_ref, wg_ref, wu_ref, wd_ref, o_ref, h_scr):
    @pl.when(pl.program_id(1) == 0)
    def _():
        x = x_ref[...]
        h_scr[...] = _rms(x, g_ref[...]).astype(BF16)
        o_ref[...] = x

    h = h_scr[...]
    gate = _dot(h, wg_ref[...])
    up = _dot(h, wu_ref[...])
    act = (gate * jax.nn.sigmoid(gate) * up).astype(BF16)
    o_ref[...] += _dot(act, wd_ref[...])


def _ffn(x, g, w_gu, w_down):
    m, d = x.shape
    f = w_down.shape[0]
    tf = 256
    nf = f // tf
    tm = _row_tile(m, 1024)
    return pl.pallas_call(
        _ffn_kernel,
        grid=(m // tm, nf),
        in_specs=[
            pl.BlockSpec((tm, d), lambda i, j: (i, 0)),
            pl.BlockSpec((1, d), lambda i, j: (0, 0)),
            pl.BlockSpec((d, tf), lambda i, j: (0, j)),
            pl.BlockSpec((d, tf), lambda i, j: (0, j + nf)),
            pl.BlockSpec((tf, d), lambda i, j: (j, 0)),
        ],
        out_specs=pl.BlockSpec((tm, d), lambda i, j: (i, 0)),
        out_shape=jax.ShapeDtypeStruct((m, d), F32),
        scratch_shapes=[pltpu.VMEM((tm, d), BF16)],
        compiler_params=_cp("parallel", "arbitrary"),
        name="ffn",
    )(x, g.reshape(1, d), w_gu, w_gu, w_down)


def _linear_res_kernel(a_ref, w_ref, b_ref, r_ref, o_ref):
    o_ref[...] = r_ref[...] + b_ref[...] + _dot(a_ref[...].astype(BF16), w_ref[...])


def _linear_res(a, w, bias, res):
    m, k = a.shape
    n = w.shape[1]
    tm = _row_tile(m, 1024)
    return pl.pallas_call(
        _linear_res_kernel,
        grid=(m // tm,),
        in_specs=[
            pl.BlockSpec((tm, k), lambda i: (i, 0)),
            pl.BlockSpec((k, n), lambda i: (0, 0)),
            pl.BlockSpec((1, n), lambda i: (0, 0)),
            pl.BlockSpec((tm, n), lambda i: (i, 0)),
        ],
        out_specs=pl.BlockSpec((tm, n), lambda i: (i, 0)),
        out_shape=jax.ShapeDtypeStruct((m, n), F32),
        compiler_params=_cp("parallel"),
        name="linear_res",
    )(a, w, bias.reshape(1, n), res)


def _nsa_proj_kernel(x_ref, g_ref, w_ref, qg_ref, kg_ref, cos_ref, sin_ref, bd_ref,
                     qn_o, qr_o, gate_o, rows_o, wrows_o, ksel_o, vsel_o, kwin_o, vwin_o):
    d = x_ref.shape[-1]
    h = _rms(x_ref[...], g_ref[...]).astype(BF16)
    proj = _dot(h, w_ref[...])
    bd = bd_ref[...]
    cos = cos_ref[...]
    sin = sin_ref[...]
    cos_q = jnp.concatenate([cos] * (d // LANES), axis=1)
    sin_q = jnp.concatenate([sin] * (d // LANES), axis=1)
    cos_k = jnp.concatenate([cos] * (KV_W // LANES), axis=1)
    sin_k = jnp.concatenate([sin] * (KV_W // LANES), axis=1)

    def head_norm(v, gain):
        return v * lax.rsqrt(_seg_sum64(v * v, bd) * (1.0 / HEAD_DIM) + RMS_EPS) * gain

    q = head_norm(proj[:, :d], qg_ref[...])
    qn_o[...] = (q * SCALE).astype(BF16)
    qr_o[...] = (_rope(q, cos_q, sin_q) * SCALE).astype(BF16)
    kv = [proj[:, d + i * KV_W:d + (i + 1) * KV_W] for i in range(6)]
    k_sel = _rope(head_norm(kv[2], kg_ref[1:2, :]), cos_k, sin_k)
    k_win = _rope(head_norm(kv[4], kg_ref[2:3, :]), cos_k, sin_k)
    rows_o[...] = jnp.concatenate([kv[0], kv[1], k_sel, kv[3]], axis=1)
    wrows_o[...] = jnp.concatenate([k_win, kv[5]], axis=1)
    ksel_o[...] = _dup64(k_sel).astype(BF16)
    vsel_o[...] = _dup64(kv[3]).astype(BF16)
    kwin_o[...] = _dup64(k_win).astype(BF16)
    vwin_o[...] = _dup64(kv[5]).astype(BF16)
    gate_o[...] = jax.nn.sigmoid(proj[:, d + 6 * KV_W:])


def _nsa_proj(x, g, w_in_p, q_gain, k_gain, cos_t, sin_t, bd, rows_per_seq):
    m, d = x.shape
    n = w_in_p.shape[1]
    tm = _row_tile(rows_per_seq, 256)
    tpb = rows_per_seq // tm
    row = lambda w: pl.BlockSpec((tm, w), lambda i: (i, 0))
    full = lambda a: pl.BlockSpec(a.shape, lambda i: (0,) * a.ndim)
    tab = pl.BlockSpec((tm, LANES), lambda i: (i % tpb, 0))
    outs = [(d, BF16), (d, BF16), (LANES, F32), (4 * KV_W, F32), (2 * KV_W, F32),
            (2 * KV_W, BF16), (2 * KV_W, BF16), (2 * KV_W, BF16), (2 * KV_W, BF16)]
    return pl.pallas_call(
        _nsa_proj_kernel,
        grid=(m // tm,),
        in_specs=[row(d), full(g), full(w_in_p), full(q_gain), full(k_gain), tab, tab, full(bd)],
        out_specs=[row(w) for w, _ in outs],
        out_shape=[jax.ShapeDtypeStruct((m, w), dt) for w, dt in outs],
        compiler_params=_cp("parallel"),
        name="nsa_proj",
    )(x, g, w_in_p, q_gain, k_gain, cos_t, sin_t, bd)


def _compress_core(xload, n_seg, n_cmp, pex_ref, w1p_ref, w2p_ref, kg_ref, bd_ref, kc_o, vc_o):
    outs = []
    for s in range(2):
        acc_pe = jnp.zeros((8, 4 * LANES), F32)
        for l in range(CMP_STRIDE):
            acc_pe += _dot(pex_ref[s, l].astype(BF16), w1p_ref[s, l])
        hid0 = acc_pe[0:1, 0:LANES] + acc_pe[1:2, LANES:2 * LANES]
        halves = []
        for p in range(2):
            acc = jnp.zeros((n_seg, 4 * LANES), F32)
            for l in range(CMP_STRIDE):
                acc += _dot(xload(s, p, l).astype(BF16), w1p_ref[s, l])
            hs = []
            for g2 in range(2):
                a0 = acc[:, (2 * g2) * LANES:(2 * g2 + 1) * LANES]
                a1 = acc[:, (2 * g2 + 1) * LANES:(2 * g2 + 2) * LANES]
                hid = a0 + pltpu.roll(a1, n_seg - 1, 0) + hid0
                hs.append(hid * jax.nn.sigmoid(hid))
            halves.append(_dot(jnp.concatenate(hs, axis=1).astype(BF16), w2p_ref[s]))
        outs.append(jnp.concatenate(halves, axis=1))
    k, v = outs
    k = k * lax.rsqrt(_seg_sum64(k * k, bd_ref[...]) * (1.0 / HEAD_DIM) + RMS_EPS) * kg_ref[0:1, :]
    valid = lax.broadcasted_iota(jnp.int32, (n_seg, 2 * KV_W), 0) < n_cmp
    kc_o[0] = jnp.where(valid, _dup64(k), 0.0).astype(BF16)
    vc_o[0] = jnp.where(valid, _dup64(v), 0.0).astype(BF16)


def _compress_prompt_kernel(x0_ref, x1_ref, x2_ref, x3_ref, pex_ref, w1p_ref, w2p_ref, kg_ref, bd_ref,
                            kc_o, vc_o, *, n_seg, n_cmp):
    xs = (x0_ref, x1_ref, x2_ref, x3_ref)

    def xload(s, p, l):
        return xs[2 * s + p][0, pl.ds(l, n_seg, stride=CMP_STRIDE), :]

    _compress_core(xload, n_seg, n_cmp, pex_ref, w1p_ref, w2p_ref, kg_ref, bd_ref, kc_o, vc_o)


def _compress_prompt(rows3, pex, w1p, w2p, k_gain, bd):
    b, t, _ = rows3.shape
    n_seg = t // CMP_STRIDE
    n_cmp = (t - CMP_LEN) // CMP_STRIDE + 1
    full = lambda a: pl.BlockSpec(a.shape, lambda i: (0,) * a.ndim)
    out = pl.BlockSpec((1, n_seg, 2 * KV_W), lambda i: (i, 0, 0))
    return pl.pallas_call(
        functools.partial(_compress_prompt_kernel, n_seg=n_seg, n_cmp=n_cmp),
        grid=(b,),
        in_specs=[pl.BlockSpec((1, t, LANES), functools.partial(lambda i, c: (i, 0, c), c=c)) for c in range(4)]
        + [full(pex), full(w1p), full(w2p), full(k_gain), full(bd)],
        out_specs=[out, out],
        out_shape=[jax.ShapeDtypeStruct((b, n_seg, 2 * KV_W), BF16)] * 2,
        compiler_params=_cp("parallel"),
        name="nsa_compress_prompt",
    )(rows3, rows3, rows3, rows3, pex, w1p, w2p, k_gain, bd)


PAGES_PER_STEP = 8


def _compress_sample_kernel(pt_ref, *refs, n_seg, n_cmp, n_steps):
    pages = refs[:PAGES_PER_STEP]
    pex_ref, w1p_ref, w2p_ref, kg_ref, bd_ref, kc_o, vc_o, x_scr = refs[PAGES_PER_STEP:]
    i = pl.program_id(1)
    for k in range(PAGES_PER_STEP):
        r0 = pl.multiple_of((i * PAGES_PER_STEP + k) * PAGE_SIZE, PAGE_SIZE)
        for c in range(4):
            x_scr[c, pl.ds(r0, PAGE_SIZE), :] = pages[k][0, 0, :, c * LANES:(c + 1) * LANES]

    @pl.when(i == n_steps - 1)
    def _():
        def xload(s, p, l):
            return x_scr[2 * s + p, pl.ds(l, n_seg, stride=CMP_STRIDE), :]

        _compress_core(xload, n_seg, n_cmp, pex_ref, w1p_ref, w2p_ref, kg_ref, bd_ref, kc_o, vc_o)


def _compress_sample(cache4, layer, page_table, pex, w1p, w2p, k_gain, bd):
    b, n_pages = page_table.shape
    past = n_pages * PAGE_SIZE
    n_seg = past // CMP_STRIDE
    n_cmp = (past + 1 - CMP_LEN) // CMP_STRIDE + 1
    n_steps = n_pages // PAGES_PER_STEP
    full = lambda a: pl.BlockSpec(a.shape, lambda bi, i, pt: (0,) * a.ndim)

    def page_spec(k):
        return pl.BlockSpec((1, 1, PAGE_SIZE, 2 * KV_W),
                            lambda bi, i, pt: (layer, pt[bi * n_pages + i * PAGES_PER_STEP + k], 0, 0))

    out = pl.BlockSpec((1, n_seg, 2 * KV_W), lambda bi, i, pt: (bi, 0, 0))
    return pl.pallas_call(
        functools.partial(_compress_sample_kernel, n_seg=n_seg, n_cmp=n_cmp, n_steps=n_steps),
        grid_spec=pltpu.PrefetchScalarGridSpec(
            num_scalar_prefetch=1,
            grid=(b, n_steps),
            in_specs=[page_spec(k) for k in range(PAGES_PER_STEP)]
            + [full(pex), full(w1p), full(w2p), full(k_gain), full(bd)],
            out_specs=[out, out],
            scratch_shapes=[pltpu.VMEM((4, past, LANES), F32)],
        ),
        out_shape=[jax.ShapeDtypeStruct((b, n_seg, 2 * KV_W), BF16)] * 2,
        compiler_params=_cp("parallel", "arbitrary"),
        name="nsa_compress_sample",
    )(page_table.reshape(-1), *([cache4] * PAGES_PER_STEP), pex, w1p, w2p, k_gain, bd)


def _masked_softmax_parts(s, mask):
    s = jnp.where(mask, s, NEG)
    m = jnp.max(s, axis=-1, keepdims=True)
    m = jnp.where(m > 0.5 * NEG, m, 0.0)
    e = jnp.where(mask, jnp.exp(s - m), 0.0)
    d = jnp.sum(e, axis=-1, keepdims=True)
    return e, jnp.where(d > 0.0, d, 1.0)


def _topk_mask(score, adm, forced, n_blk):
    blk = _lane_iota(score.shape)
    sc = jnp.where(forced, jnp.inf, jnp.where(adm, score, -jnp.inf))
    rank = jnp.zeros(score.shape, F32)
    for i in range(n_blk):
        ci = sc[:, i:i + 1]
        rank += jnp.where(ci > sc, 1.0, jnp.where(ci == sc, jnp.where(blk > i, 1.0, 0.0), 0.0))
    return jnp.where(adm, jnp.where(rank < N_SEL, 1.0, 0.0), 0.0)


def _nsa_attn_kernel(qn_ref, qr_ref, gate_ref, kc_ref, vc_ref, ks_ref, vs_ref, kw_ref, vw_ref,
                     msel_ref, esel_ref, eg_ref, o_ref, *, tq, t_len, n_slc):
    t0 = pl.program_id(2) * tq
    t = t0 + lax.broadcasted_iota(jnp.int32, (tq, 1), 0)
    lo = _lane_iota((tq, LANES)) < HEAD_DIM
    kc = kc_ref[0]
    vc = vc_ref[0]
    ncp = kc.shape[0]
    cmp_mask = (lax.broadcasted_iota(jnp.int32, (tq, ncp), 1) * CMP_STRIDE + (CMP_LEN - 1)) <= t

    def q_heads(ref):
        out = []
        for p in range(2):
            qp = ref[:, p * LANES:(p + 1) * LANES]
            zero = jnp.zeros_like(qp)
            out.append((jnp.where(lo, qp, zero), jnp.where(lo, zero, qp)))
        return out

    qn = q_heads(qn_ref)
    qr = q_heads(qr_ref)

    psum = jnp.zeros((tq, ncp), F32)
    o_cmp = []
    for p in range(2):
        halves = []
        for hf in range(2):
            e, d = _masked_softmax_parts(_dot_nt(qn[p][hf], kc), cmp_mask)
            pn = e / d
            psum += pn
            halves.append(_dot(pn.astype(BF16), vc))
        o_cmp.append(jnp.where(lo, halves[0], halves[1]))
    p_hi, p_lo = _split(psum)
    score = _dot(p_hi, msel_ref[...]) + _dot(p_lo, msel_ref[...])
    blk = _lane_iota(score.shape)
    cur = t // SEL_BLOCK
    adm = blk * SEL_BLOCK <= t
    forced = adm & ((blk == 0) | (blk == cur) | (blk == cur - 1))
    sel = _topk_mask(score, adm, forced, n_slc)

    key = lax.broadcasted_iota(jnp.int32, (tq, t_len), 1)
    sel_keys = _dot(sel.astype(BF16), esel_ref[...])
    allow = jnp.where(key <= t, sel_keys, 0.0) > 0.5
    ks = ks_ref[0]
    vs = vs_ref[0]
    o_sel = []
    for p in range(2):
        halves = []
        for hf in range(2):
            e, d = _masked_softmax_parts(_dot_nt(qr[p][hf], ks), allow)
            halves.append(_dot(e.astype(BF16), vs) / d)
        o_sel.append(jnp.where(lo, halves[0], halves[1]))

    wl = min(WINDOW + tq, t_len)
    start = pl.multiple_of(jnp.minimum(jnp.maximum(t0 - WINDOW, 0), t_len - wl), tq)
    kw = kw_ref[0, pl.ds(start, wl), :]
    vw = vw_ref[0, pl.ds(start, wl), :]
    keyw = start + lax.broadcasted_iota(jnp.int32, (tq, wl), 1)
    m_w = (keyw <= t) & (keyw > t - WINDOW)
    o_win = []
    for p in range(2):
        halves = []
        for hf in range(2):
            e, d = _masked_softmax_parts(_dot_nt(qr[p][hf], kw), m_w)
            halves.append(_dot(e.astype(BF16), vw) / d)
        o_win.append(jnp.where(lo, halves[0], halves[1]))

    g_hi, g_lo = _split(gate_ref[...])
    ge = [_dot(g_hi, eg_ref[0, b]) + _dot(g_lo, eg_ref[0, b]) for b in range(3)]
    outs = []
    for p in range(2):
        sl = slice(p * LANES, (p + 1) * LANES)
        outs.append(ge[0][:, sl] * o_cmp[p] + ge[1][:, sl] * o_sel[p] + ge[2][:, sl] * o_win[p])
    o_ref[...] = jnp.concatenate(outs, axis=1).astype(o_ref.dtype)


def _nsa_attn_prompt(qn, qr, gates, kc, vc, ks, vs, kw, vw, msel, esel, eg, b, t_len):
    m, d = qn.shape
    tq = _row_tile(t_len, 128)
    nq = t_len // tq
    n_seg = kc.shape[1]
    n_slc = t_len // SEL_BLOCK
    gw = GROUP * HEAD_DIM
    qspec = pl.BlockSpec((tq, gw), lambda bi, g, qi: (bi * nq + qi, g))
    kv_c = pl.BlockSpec((1, n_seg, LANES), lambda bi, g, qi: (bi, 0, g))
    kv_t = pl.BlockSpec((1, t_len, LANES), lambda bi, g, qi: (bi, 0, g))
    full = lambda a: pl.BlockSpec(a.shape, lambda bi, g, qi: (0,) * a.ndim)
    return pl.pallas_call(
        functools.partial(_nsa_attn_kernel, tq=tq, t_len=t_len, n_slc=n_slc),
        grid=(b, N_KV_HEADS, nq),
        in_specs=[qspec, qspec, pl.BlockSpec((tq, LANES), lambda bi, g, qi: (bi * nq + qi, 0)),
                  kv_c, kv_c, kv_t, kv_t, kv_t, kv_t, full(msel), full(esel),
                  pl.BlockSpec((1, 3, LANES, gw), lambda bi, g, qi: (g, 0, 0, 0))],
        out_specs=qspec,
        out_shape=jax.ShapeDtypeStruct((m, d), BF16),
        compiler_params=_cp("parallel", "parallel", "arbitrary"),
        name="nsa_attn_prompt",
    )(qn, qr, gates, kc, vc, ks.reshape(b, t_len, -1), vs.reshape(b, t_len, -1),
      kw.reshape(b, t_len, -1), vw.reshape(b, t_len, -1), msel, esel, eg)


def _head_rows(q_row, n_heads):
    qb = jnp.broadcast_to(q_row.astype(F32), (n_heads, q_row.shape[-1]))
    row = lax.broadcasted_iota(jnp.int32, (n_heads, LANES), 0)
    lane = _lane_iota((n_heads, LANES))
    out = jnp.zeros((n_heads, LANES), F32)
    for j in range(n_heads // 2):
        out = jnp.where(row // 2 == j, qb[:, j * LANES:(j + 1) * LANES], out)
    return jnp.where(lane // HEAD_DIM == row % 2, out, 0.0).astype(q_row.dtype)


def _sample_cmp_kernel(qn_ref, kc_ref, vc_ref, msel_ref, ocmp_o, idx_o, *, n_cmp, n_slc, n_heads):
    q = _head_rows(qn_ref[0], n_heads)
    ncp = kc_ref.shape[1]
    row = lax.broadcasted_iota(jnp.int32, (n_heads, 1), 0)
    s = jnp.zeros((n_heads, ncp), F32)
    for g in range(N_KV_HEADS):
        s = jnp.where(row // GROUP == g, _dot_nt(q, kc_ref[0, :, g * LANES:(g + 1) * LANES]), s)
    e, d = _masked_softmax_parts(s, lax.broadcasted_iota(jnp.int32, (n_heads, ncp), 1) < n_cmp)
    pn = e / d
    o = jnp.zeros((n_heads, LANES), F32)
    for g in range(N_KV_HEADS):
        o = jnp.where(row // GROUP == g, _dot(pn.astype(BF16), vc_ref[0, :, g * LANES:(g + 1) * LANES]), o)
    ocmp_o[0] = o
    gsel = (lax.broadcasted_iota(jnp.int32, (8, n_heads), 1) // GROUP
            == lax.broadcasted_iota(jnp.int32, (8, n_heads), 0)).astype(F32)
    psum = _dot_hp(gsel, pn)
    p_hi, p_lo = _split(psum)
    score = _dot(p_hi, msel_ref[...]) + _dot(p_lo, msel_ref[...])
    blk = _lane_iota(score.shape)
    blkf = blk.astype(F32)
    cur = n_slc - 1
    adm = blk < n_slc
    forced = (blk == 0) | (blk == cur) | (blk == cur - 1)
    sc = jnp.where(forced, jnp.inf, jnp.where(adm, score, -jnp.inf))
    lane_k = _lane_iota((8, LANES))
    idx = jnp.full((8, LANES), -1.0, F32)
    for k in range(N_SEL):
        mx = jnp.max(sc, axis=-1, keepdims=True)
        ik = jnp.min(jnp.where(sc == mx, blkf, 1e9), axis=-1, keepdims=True)
        idx = jnp.where(lane_k == k, jnp.where(mx > -jnp.inf, ik, -1.0), idx)
        sc = jnp.where(blkf == ik, -jnp.inf, sc)
    idx_o[0] = idx.astype(jnp.int32)


def _sample_cmp(qn3, kc, vc, msel, n_cmp, n_slc):
    b, _, d = qn3.shape
    n_heads = d // HEAD_DIM
    ncp = kc.shape[1]
    return pl.pallas_call(
        functools.partial(_sample_cmp_kernel, n_cmp=n_cmp, n_slc=n_slc, n_heads=n_heads),
        grid=(b,),
        in_specs=[pl.BlockSpec((1, 1, d), lambda i: (i, 0, 0)),
                  pl.BlockSpec((1, ncp, 2 * KV_W), lambda i: (i, 0, 0)),
                  pl.BlockSpec((1, ncp, 2 * KV_W), lambda i: (i, 0, 0)),
                  pl.BlockSpec(msel.shape, lambda i: (0, 0))],
        out_specs=[pl.BlockSpec((1, n_heads, LANES), lambda i: (i, 0, 0)),
                   pl.BlockSpec((1, 8, LANES), lambda i: (i, 0, 0))],
        out_shape=[jax.ShapeDtypeStruct((b, n_heads, LANES), F32),
                   jax.ShapeDtypeStruct((b, 8, LANES), jnp.int32)],
        compiler_params=_cp("parallel"),
        name="nsa_sample_cmp",
    )(qn3, kc, vc, msel)


def _sample_attn_kernel(idx_ref, pt_ref, *refs, n_past_blk):
    blocks = refs[:N_SEL]
    qr_ref, rows_ref, win_ref, gate_ref, ocmp_ref, o_ref, k_scr, v_scr = refs[N_SEL:]
    b = pl.program_id(0)
    g = pl.program_id(1)
    hi_half = (g % 2) == 1
    tile_hi = (g // 2) == 1
    lane = _lane_iota((GROUP, LANES))
    in_half = lane // HEAD_DIM == g % 2

    q_all = qr_ref[0]
    rows_q = []
    for n in range(GROUP):
        cands = []
        for gg in range(N_KV_HEADS):
            h = gg * GROUP + n
            cands.append(q_all[:, (h // 2) * LANES:(h // 2 + 1) * LANES].astype(F32))
        tile = jnp.where(g == 0, cands[0], jnp.where(g == 1, cands[1], jnp.where(g == 2, cands[2], cands[3])))
        rolled = pltpu.roll(tile, HEAD_DIM, 1)
        rows_q.append(jnp.where(hi_half if n % 2 == 1 else jnp.logical_not(hi_half), tile, rolled))
    q = jnp.where(in_half, jnp.concatenate(rows_q, axis=0), 0.0).astype(BF16)

    def pick_tile(x2d, base):
        a = x2d[:, base:base + LANES]
        c = x2d[:, base + LANES:base + 2 * LANES]
        return jnp.where(tile_hi, c, a)

    valid = jnp.zeros((1, N_SEL * SEL_BLOCK), F32)
    keyblk = _lane_iota((1, N_SEL * SEL_BLOCK)) // SEL_BLOCK
    for k in range(N_SEL):
        blkd = blocks[k][0, 0]
        k_scr[k * SEL_BLOCK:(k + 1) * SEL_BLOCK, :] = pick_tile(blkd, 0).astype(BF16)
        v_scr[k * SEL_BLOCK:(k + 1) * SEL_BLOCK, :] = pick_tile(blkd, KV_W).astype(BF16)
        ik = idx_ref[(b * N_KV_HEADS + g) * N_SEL + k]
        ok = jnp.where((ik >= 0) & (ik < n_past_blk), 1.0, 0.0)
        valid = jnp.where(keyblk == k, ok, valid)
    s_sel = _dot_nt(q, k_scr[...])
    new_row = rows_ref[0]
    k_new = pick_tile(new_row, 2 * KV_W)
    v_new = pick_tile(new_row, 3 * KV_W)
    s_new = jnp.sum(q.astype(F32) * k_new.astype(BF16).astype(F32), axis=-1, keepdims=True)
    s_m = jnp.where(valid > 0.5, s_sel, NEG)
    mx = jnp.maximum(jnp.max(s_m, axis=-1, keepdims=True), s_new)
    e = jnp.where(valid > 0.5, jnp.exp(s_m - mx), 0.0)
    e_new = jnp.exp(s_new - mx)
    den = jnp.sum(e, axis=-1, keepdims=True) + e_new
    o_sel = (_dot(e.astype(BF16), v_scr[...]) + e_new.astype(BF16).astype(F32) * v_new.astype(BF16).astype(F32)) / den

    wb = win_ref.shape[1]
    kw = pick_tile(win_ref[0], 0).astype(BF16)
    vw = pick_tile(win_ref[0], KV_W).astype(BF16)
    s_w = _dot_nt(q, kw)
    mw = jnp.max(s_w, axis=-1, keepdims=True)
    ew = jnp.exp(s_w - mw)
    o_win = _dot(ew.astype(BF16), vw) / jnp.sum(ew, axis=-1, keepdims=True)

    oc = ocmp_ref[0]
    o_cmp = jnp.where(g == 0, oc[0:4], jnp.where(g == 1, oc[4:8], jnp.where(g == 2, oc[8:12], oc[12:16])))

    gt = jnp.broadcast_to(gate_ref[0], (GROUP, LANES))
    hrow = g * GROUP + lax.broadcasted_iota(jnp.int32, (GROUP, LANES), 0)
    gv = [jnp.sum(jnp.where(lane == hrow * 3 + br, gt, 0.0), axis=-1, keepdims=True) for br in range(3)]
    o_ref[0, 0] = gv[0] * o_cmp + gv[1] * o_sel + gv[2] * o_win


def _sample_attn(idx, page_table, cache_blk, layer, qr3, rows3, win_new, gates3, ocmp, n_past_blk):
    b, n_pages = page_table.shape
    d = qr3.shape[-1]
    wb = win_new.shape[1]
    per_page = PAGE_SIZE // SEL_BLOCK

    def blk_spec(k):
        def imap(bi, g, idx_r, pt_r):
            ik = idx_r[(bi * N_KV_HEADS + g) * N_SEL + k]
            ip = jnp.minimum(jnp.maximum(ik, 0), n_past_blk - 1)
            page = pt_r[bi * n_pages + ip // per_page]
            return (layer, page * per_page + ip % per_page, 0, 1)
        return pl.BlockSpec((1, 1, SEL_BLOCK, 2 * KV_W), imap)

    per_b = lambda shp: pl.BlockSpec((1,) + shp, lambda bi, g, i_r, p_r: (bi,) + (0,) * len(shp))
    return pl.pallas_call(
        functools.partial(_sample_attn_kernel, n_past_blk=n_past_blk),
        grid_spec=pltpu.PrefetchScalarGridSpec(
            num_scalar_prefetch=2,
            grid=(b, N_KV_HEADS),
            in_specs=[blk_spec(k) for k in range(N_SEL)]
            + [per_b((1, d)), per_b((1, 4 * KV_W)), per_b((wb, 2 * KV_W)), per_b((1, LANES)),
               per_b((d // HEAD_DIM, LANES))],
            out_specs=pl.BlockSpec((1, 1, GROUP, LANES), lambda bi, g, i_r, p_r: (bi, g, 0, 0)),
            scratch_shapes=[pltpu.VMEM((N_SEL * SEL_BLOCK, LANES), BF16)] * 2,
        ),
        out_shape=jax.ShapeDtypeStruct((b, N_KV_HEADS, GROUP, LANES), F32),
        compiler_params=_cp("parallel", "arbitrary"),
        name="nsa_sample_attn",
    )(idx.reshape(-1), page_table.reshape(-1), *([cache_blk] * N_SEL), qr3, rows3, win_new, gates3, ocmp)


def _rwkv_proj_kernel(*refs, tiles_per_seq, per_row_prev):
    if per_row_prev:
        x_ref, prev_ref = refs[:2]
        rest = refs[2:]
    else:
        x_ref, xp_ref, shift_ref = refs[:3]
        rest = refs[3:]
    (g_ref, mix_ref, wrkv_ref, w0_ref, w1_ref, w2_ref, a0_ref, a1_ref, a2_ref, g1_ref, g2_ref,
     kk_ref, ka_ref, bd_ref, r_o, lw_o, k_o, v_o, kk_o, b_o, g_o, h_o) = rest
    gain = g_ref[...]
    h = _rms(x_ref[...], gain)
    if per_row_prev:
        prev = prev_ref[...]
        h_o[...] = h
    else:
        h_o[0] = h[h.shape[0] - 8:, :]
        tm = h.shape[0]
        hp = _rms(xp_ref[...], gain)[7:8, :]
        first = (pl.program_id(0) % tiles_per_seq) == 0
        row0 = jnp.where(first, shift_ref[0], hp)
        prev = jnp.where(lax.broadcasted_iota(jnp.int32, h.shape, 0) == 0, row0, pltpu.roll(h, 1, 0))
    dlt = prev - h
    xm = [(h + dlt * mix_ref[s:s + 1, :]).astype(BF16) for s in range(6)]
    r = _dot(xm[0], wrkv_ref[0])
    k = _dot(xm[1], wrkv_ref[1])
    v = _dot(xm[2], wrkv_ref[2])
    z = w0_ref[...] + _dot(jnp.tanh(_dot(xm[3], w1_ref[...])).astype(BF16), w2_ref[...])
    u = -z
    softplus = jnp.maximum(u, 0.0) + jnp.log1p(jnp.exp(-jnp.abs(u)))
    lw_o[...] = -jnp.exp(-softplus - 0.5)
    a = jax.nn.sigmoid(a0_ref[...] + _dot(_dot(xm[4], a1_ref[...]).astype(BF16), a2_ref[...]))
    g_o[...] = _dot(jax.nn.sigmoid(_dot(xm[5], g1_ref[...])).astype(BF16), g2_ref[...])
    kk = k * kk_ref[...]
    kk = kk * lax.rsqrt(jnp.maximum(_seg_sum64(kk * kk, bd_ref[...], two_pass=True), 1e-24))
    r_o[...] = r
    v_o[...] = v
    k_o[...] = k * (1.0 + (a - 1.0) * ka_ref[...])
    kk_o[...] = kk
    b_o[...] = kk * a


def _rwkv_proj(x, prev, shift, rows_per_seq, g, wts, bd):
    m, d = x.shape
    tm = _row_tile(rows_per_seq if prev is None else m, 256)
    tps = max(rows_per_seq // tm, 1)
    row = pl.BlockSpec((tm, d), lambda i: (i, 0))
    full = lambda a: pl.BlockSpec(a.shape, lambda i: (0,) * a.ndim)
    if prev is None:
        lead = [x, x, shift]
        lead_specs = [row, pl.BlockSpec((8, d), lambda i: (jnp.maximum(i * (tm // 8) - 1, 0), 0)),
                      pl.BlockSpec((1, 1, d), lambda i: (i // tps, 0, 0))]
    else:
        lead = [x, prev]
        lead_specs = [row, row]
    ws = [g, wts["mix"], wts["w_rkv"], wts["w0"], wts["w1"], wts["w2"], wts["a0"], wts["a1"], wts["a2"],
          wts["g1"], wts["g2"], wts["k_k"], wts["k_a"], bd]
    if prev is None:
        h_spec = pl.BlockSpec((1, 8, d), lambda i: (i // tps, 0, 0))
        h_shape = jax.ShapeDtypeStruct((m // rows_per_seq, 8, d), F32)
    else:
        h_spec, h_shape = row, jax.ShapeDtypeStruct((m, d), F32)
    return pl.pallas_call(
        functools.partial(_rwkv_proj_kernel, tiles_per_seq=tps, per_row_prev=prev is not None),
        grid=(m // tm,),
        in_specs=lead_specs + [full(a) for a in ws],
        out_specs=[row] * 7 + [h_spec],
        out_shape=[jax.ShapeDtypeStruct((m, d), F32)] * 7 + [h_shape],
        compiler_params=_cp("arbitrary"),
        name="rwkv_proj",
    )(*lead, *ws)


def _stack2(x, lo):
    z = jnp.zeros_like(x)
    return jnp.concatenate([jnp.where(lo, x, z), jnp.where(lo, z, x)], axis=0)


def _rwkv_scan_kernel(r_ref, lw_ref, k_ref, v_ref, kk_ref, b_ref, y_o, s_o, s_scr, *, tt, n_pairs):
    c = RWKV_CHUNK

    @pl.when(pl.program_id(1) == 0)
    def _():
        s_scr[...] = jnp.zeros_like(s_scr)

    lo = _lane_iota((c, LANES)) < RWKV_HEAD
    ri = lax.broadcasted_iota(jnp.int32, (2 * c, 2 * c), 0)
    ci = lax.broadcasted_iota(jnp.int32, (2 * c, 2 * c), 1)
    strict = (ri % c) > (ci % c)
    incl = (ri % c) >= (ci % c)
    eye = (ri == ci).astype(F32)
    ltri = (lax.broadcasted_iota(jnp.int32, (c, c), 0) >= lax.broadcasted_iota(jnp.int32, (c, c), 1)).astype(F32)

    def chunk(ic, carry):
        rows = pl.ds(pl.multiple_of(ic * c, c), c)
        for p in range(n_pairs):
            lanes = slice(p * LANES, (p + 1) * LANES)
            lw = lw_ref[rows, lanes]
            cum = _dot_hp(ltri, lw)
            g_in = jnp.exp(cum)
            g_ex = jnp.exp(cum - lw)
            g_inv = jnp.exp(-cum)
            g_end = g_in[c - 1:c, :]
            kk = kk_ref[rows, lanes]
            a_s = _stack2(-kk * g_ex, lo).astype(BF16)
            r_s = _stack2(r_ref[rows, lanes] * g_in, lo).astype(BF16)
            bt = b_ref[rows, lanes] * g_inv
            kt = k_ref[rows, lanes] * g_inv
            b_s = _stack2(bt, lo).astype(BF16)
            k_s = _stack2(kt, lo).astype(BF16)
            v_s = _stack2(v_ref[rows, lanes], lo)
            s0 = s_scr[p]
            s0b = s0.astype(BF16)
            n_ab = jnp.where(strict, _dot_nt(a_s, b_s), 0.0)
            n_ak = jnp.where(strict, _dot_nt(a_s, k_s), 0.0)
            m_rb = jnp.where(incl, _dot_nt(r_s, b_s), 0.0)
            m_rk = jnp.where(incl, _dot_nt(r_s, k_s), 0.0)
            v_sb = v_s.astype(BF16)
            rhs = _dot_nt(a_s, s0b) + _dot(n_ak.astype(BF16), v_sb)
            pw = n_ab
            tinv = eye + pw
            for _ in range(3):
                pw = _dot_hp(pw, pw)
                tinv = tinv + _dot_hp(pw, tinv)
            u_s = _dot_hp(tinv, rhs)
            u_sb = u_s.astype(BF16)
            y_s = _dot_nt(r_s, s0b) + _dot(m_rb.astype(BF16), u_sb) + _dot(m_rk.astype(BF16), v_sb)
            y_o[rows, lanes] = y_s[:c] + y_s[c:]
            bg = _stack2(bt * g_end, lo).astype(BF16)
            kg = _stack2(kt * g_end, lo).astype(BF16)
            s_scr[p] = (s0 * g_end + _dot_tn(jnp.concatenate([u_sb, v_sb], axis=0),
                                            jnp.concatenate([bg, kg], axis=0)))
        return carry

    lax.fori_loop(0, tt // c, chunk, 0)

    @pl.when(pl.program_id(1) == pl.num_programs(1) - 1)
    def _():
        s_o[0] = s_scr[...]


def _rwkv_scan(r, lw, k, v, kk, bb, b, t_len):
    m, d = r.shape
    n_pairs = d // LANES
    tt = _row_tile(t_len, 128)
    nt = t_len // tt
    row = pl.BlockSpec((tt, d), lambda bi, ti: (bi * nt + ti, 0))
    return pl.pallas_call(
        functools.partial(_rwkv_scan_kernel, tt=tt, n_pairs=n_pairs),
        grid=(b, nt),
        in_specs=[row] * 6,
        out_specs=[row, pl.BlockSpec((1, n_pairs, LANES, LANES), lambda bi, ti: (bi, 0, 0, 0))],
        out_shape=[jax.ShapeDtypeStruct((m, d), F32),
                   jax.ShapeDtypeStruct((b, n_pairs, LANES, LANES), F32)],
        scratch_shapes=[pltpu.VMEM((n_pairs, LANES, LANES), F32)],
        compiler_params=_cp("parallel", "arbitrary"),
        name="rwkv_scan",
    )(r, lw, k, v, kk, bb)


def _rwkv_step_kernel(s_ref, r_ref, w_ref, k_ref, kk_ref, b_ref, v_ref, y_o, s_o):
    s = s_ref[0]
    sa = -jnp.sum(s * kk_ref[0], axis=-1, keepdims=True)
    s_new = s * jnp.exp(w_ref[0]) + sa * b_ref[0] + v_ref[0] * k_ref[0]
    s_o[0] = s_new
    y_o[0] = jnp.sum(s_new * r_ref[0], axis=-1, keepdims=True)


def _rwkv_step(state, r, lw, k, kk, bb, v):
    b, nh, n, _ = state.shape
    rowv = lambda a: a.reshape(b, nh, 1, n)
    rspec = pl.BlockSpec((1, nh, 1, n), lambda i: (i, 0, 0, 0))
    cspec = pl.BlockSpec((1, nh, n, 1), lambda i: (i, 0, 0, 0))
    sspec = pl.BlockSpec((1, nh, n, n), lambda i: (i, 0, 0, 0))
    y, s_new = pl.pallas_call(
        _rwkv_step_kernel,
        grid=(b,),
        in_specs=[sspec, rspec, rspec, rspec, rspec, rspec, cspec],
        out_specs=[cspec, sspec],
        out_shape=[jax.ShapeDtypeStruct((b, nh, n, 1), F32), jax.ShapeDtypeStruct((b, nh, n, n), F32)],
        compiler_params=_cp("parallel"),
        name="rwkv_step",
    )(state, rowv(r), rowv(lw), rowv(k), rowv(kk), rowv(bb), v.reshape(b, nh, n, 1))
    return y.reshape(b, nh * n), s_new


def _rwkv_post_kernel(y_ref, r_ref, k_ref, v_ref, g_ref, x_ref, lnw_ref, lnb_ref, rk_ref, bd_ref, w_ref, o_ref):
    bd = bd_ref[...]
    y = y_ref[...]
    mu = _seg_sum64(y, bd, two_pass=True) * (1.0 / RWKV_HEAD)
    yc = y - mu
    var = _seg_sum64(yc * yc, bd, two_pass=True) * (1.0 / RWKV_HEAD)
    yn = yc * lax.rsqrt(var + GN_EPS) * lnw_ref[...] + lnb_ref[...]
    bonus = _seg_sum64(r_ref[...] * k_ref[...] * rk_ref[...], bd, two_pass=True) * v_ref[...]
    z = ((yn + bonus) * g_ref[...]).astype(BF16)
    o_ref[...] = x_ref[...] + _dot(z, w_ref[...])


def _rwkv_post(y, r, k, v, g, x, ln_w, ln_b, r_k, bd, w_out):
    m, d = y.shape
    tm = _row_tile(m, 512)
    row = pl.BlockSpec((tm, d), lambda i: (i, 0))
    full = lambda a: pl.BlockSpec(a.shape, lambda i: (0,) * a.ndim)
    return pl.pallas_call(
        _rwkv_post_kernel,
        grid=(m // tm,),
        in_specs=[row] * 6 + [full(ln_w), full(ln_b), full(r_k), full(bd), full(w_out)],
        out_specs=row,
        out_shape=jax.ShapeDtypeStruct((m, d), F32),
        compiler_params=_cp("parallel"),
        name="rwkv_post",
    )(y, r, k, v, g, x, ln_w, ln_b, r_k, bd, w_out)


CONV_HALO = 32


def _glu(x, gain, w1_ref, b1_ref):
    d = x.shape[-1]
    u = _dot(_rms(x, gain).astype(BF16), w1_ref[...]) + b1_ref[...]
    return u[:, :d] * jax.nn.sigmoid(u[:, d:])


def _ln_silu(y, lnw, lnb):
    mu = jnp.mean(y, axis=-1, keepdims=True)
    yc = y - mu
    var = jnp.mean(yc * yc, axis=-1, keepdims=True)
    z = yc * lax.rsqrt(var + LN_EPS) * lnw + lnb
    return z * jax.nn.sigmoid(z)


def _conv_prompt_kernel(x_ref, xh_ref, buf_ref, g_ref, w1_ref, b1_ref, wdw_ref, bdw_ref, lnw_ref, lnb_ref,
                        w2_ref, b2_ref, o_ref, tail_o, ext_scr, *, tm, tiles_per_seq):
    ti = pl.program_id(1)
    gain = g_ref[...]
    x = x_ref[...]
    glu = _glu(x, gain, w1_ref, b1_ref)
    halo = _glu(xh_ref[...], gain, w1_ref, b1_ref)
    ext_scr[0:CONV_HALO, :] = jnp.where(ti == 0, buf_ref[0], halo)
    ext_scr[CONV_HALO:, :] = glu
    acc = jnp.zeros_like(glu) + bdw_ref[...]
    off = CONV_HALO - (CONV_W - 1)
    for k in range(CONV_W):
        acc += ext_scr[off + k:off + k + tm, :] * wdw_ref[k:k + 1, :]
    z = _ln_silu(acc, lnw_ref[...], lnb_ref[...]).astype(BF16)
    o_ref[...] = x + b2_ref[...] + _dot(z, w2_ref[...])

    @pl.when(ti == tiles_per_seq - 1)
    def _():
        tail_o[0] = glu[tm - CONV_HALO:, :]


def _conv_prompt(x, buf, b, t_len, g, wts):
    m, d = x.shape
    tm = _row_tile(t_len, 512)
    tps = t_len // tm
    hb = tm // CONV_HALO
    row = pl.BlockSpec((tm, d), lambda bi, ti: (bi * tps + ti, 0))
    full = lambda a: pl.BlockSpec(a.shape, lambda bi, ti: (0,) * a.ndim)
    ws = [g, wts["w_pw1"], wts["b_pw1"], wts["w_dw"], wts["b_dw"], wts["ln_w"], wts["ln_b"], wts["w_pw2"], wts["b_pw2"]]
    return pl.pallas_call(
        functools.partial(_conv_prompt_kernel, tm=tm, tiles_per_seq=tps),
        grid=(b, tps),
        in_specs=[row,
                  pl.BlockSpec((CONV_HALO, d), lambda bi, ti: (jnp.maximum((bi * tps + ti) * hb - 1, 0), 0)),
                  pl.BlockSpec((1, CONV_HALO, d), lambda bi, ti: (bi, 0, 0))] + [full(a) for a in ws],
        out_specs=[row, pl.BlockSpec((1, CONV_HALO, d), lambda bi, ti: (bi, 0, 0))],
        out_shape=[jax.ShapeDtypeStruct((m, d), F32), jax.ShapeDtypeStruct((b, CONV_HALO, d), F32)],
        scratch_shapes=[pltpu.VMEM((tm + CONV_HALO, d), F32)],
        compiler_params=_cp("parallel", "arbitrary"),
        name="conv_prompt",
    )(x, x, buf, *ws)


def _conv_sample_kernel(x_ref, buf_ref, g_ref, w1_ref, b1_ref, wdw_ref, bdw_ref, lnw_ref, lnb_ref,
                        w2_ref, b2_ref, o_ref, glu_o):
    x = x_ref[...]
    glu = _glu(x, g_ref[...], w1_ref, b1_ref)
    glu_o[...] = glu
    acc = bdw_ref[...] + glu * wdw_ref[CONV_W - 1:CONV_W, :]
    for k in range(CONV_W - 1):
        acc += buf_ref[k] * wdw_ref[k:k + 1, :]
    z = _ln_silu(acc, lnw_ref[...], lnb_ref[...]).astype(BF16)
    o_ref[...] = x + b2_ref[...] + _dot(z, w2_ref[...])


def _conv_sample(x, buf_t, g, wts):
    ws = [g, wts["w_pw1"], wts["b_pw1"], wts["w_dw"], wts["b_dw"], wts["ln_w"], wts["ln_b"], wts["w_pw2"], wts["b_pw2"]]
    full = lambda a: pl.BlockSpec(a.shape, lambda i: (0,) * a.ndim)
    return pl.pallas_call(
        _conv_sample_kernel,
        grid=(1,),
        in_specs=[full(x), full(buf_t)] + [full(a) for a in ws],
        out_specs=[full(x), full(x)],
        out_shape=[jax.ShapeDtypeStruct(x.shape, F32)] * 2,
        compiler_params=_cp("arbitrary"),
        name="conv_sample",
    )(x, buf_t, *ws)


def _rope_tables(pos):
    half = HEAD_DIM // 2
    inv = ROPE_THETA ** (-jnp.arange(half, dtype=F32) / half)
    ang = pos.astype(F32)[:, None] * inv
    cos = jnp.cos(ang)
    sin = jnp.sin(ang)
    cos128 = jnp.concatenate([cos, cos, cos, cos], axis=1)
    sin128 = jnp.concatenate([-sin, sin, -sin, sin], axis=1)
    return cos128, sin128


def _block_ones():
    i = jnp.arange(256)
    return (i[:, None] // HEAD_DIM == i[None, :] // HEAD_DIM).astype(BF16)


def _cmp_to_sel(n_rows, n_cmp, n_cols, n_slc):
    c = jnp.arange(n_rows)[:, None]
    j = jnp.arange(n_cols)[None, :]
    c0 = c * CMP_STRIDE
    j0 = j * SEL_BLOCK
    hit = (c0 < j0 + SEL_BLOCK) & (c0 + CMP_LEN > j0) & (c < n_cmp) & (j < n_slc)
    return hit.astype(BF16)


def _nsa_weights(w_in, q_norm, k_norm, cmp_pe, cmp_w1, cmp_w2, w_out, d):
    n_heads = d // HEAD_DIM
    n_gate = 3 * n_heads
    w_in_p = jnp.pad(w_in, ((0, 0), (0, LANES - n_gate))).astype(BF16)
    q_gain = jnp.tile(q_norm, n_heads).reshape(1, d)
    k_gain = jnp.tile(k_norm, (1, N_KV_HEADS))
    eye2 = jnp.eye(2, dtype=F32)
    w1r = cmp_w1.reshape(2, CMP_RATIO, CMP_STRIDE, HEAD_DIM, CMP_HIDDEN)
    base = w1r.transpose(0, 2, 3, 1, 4).reshape(2, CMP_STRIDE, HEAD_DIM, CMP_RATIO * CMP_HIDDEN)
    w1p = jnp.einsum("gh,sldx->slgdhx", eye2, base).reshape(2, CMP_STRIDE, 2 * HEAD_DIM, 2 * CMP_RATIO * CMP_HIDDEN)
    w2p = jnp.einsum("gh,sfd->sgfhd", eye2, cmp_w2).reshape(2, 2 * CMP_HIDDEN, 2 * HEAD_DIM)
    pe = cmp_pe.reshape(2, CMP_RATIO, CMP_STRIDE, HEAD_DIM).transpose(0, 2, 1, 3)
    pex = jnp.pad(pe, ((0, 0), (0, 0), (0, 8 - CMP_RATIO), (0, LANES - HEAD_DIM)))
    c = jnp.arange(LANES)[None, None, :, None]
    lane = jnp.arange(GROUP * HEAD_DIM)[None, None, None, :]
    g = jnp.arange(N_KV_HEADS)[:, None, None, None]
    br = jnp.arange(3)[None, :, None, None]
    eg = (c == (g * GROUP + lane // HEAD_DIM) * 3 + br).astype(BF16)
    return dict(w_in=w_in_p, q_gain=q_gain, k_gain=k_gain, w1p=w1p.astype(BF16), w2p=w2p.astype(BF16),
                pex=pex, eg=eg, w_out=w_out.astype(BF16))


def _nsa_prompt_layer(x, g, w, b, t_len, tabs, bd):
    m, d = x.shape
    qn, qr, gates, rows, wrows, ks, vs, kw, vw = _nsa_proj(
        x, g, w["w_in"], w["q_gain"], w["k_gain"], tabs[0], tabs[1], bd, t_len)
    kc, vc = _compress_prompt(rows.reshape(b, t_len, -1), w["pex"], w["w1p"], w["w2p"], w["k_gain"], bd)
    n_seg = t_len // CMP_STRIDE
    n_cmp = (t_len - CMP_LEN) // CMP_STRIDE + 1
    n_slc = t_len // SEL_BLOCK
    msel = _cmp_to_sel(n_seg, n_cmp, LANES, n_slc)
    esel = (jnp.arange(LANES)[:, None] == jnp.arange(t_len)[None, :] // SEL_BLOCK).astype(BF16)
    o = _nsa_attn_prompt(qn, qr, gates, kc, vc, ks, vs, kw, vw, msel, esel, w["eg"], b, t_len)
    x = _linear_res(o, w["w_out"], jnp.zeros((d,), F32), x)
    wb = min(WINDOW, t_len)
    return x, rows.reshape(b, t_len, 4, N_KV_HEADS, HEAD_DIM), \
        wrows.reshape(b, t_len, 2, N_KV_HEADS, HEAD_DIM)[:, t_len - wb:]


def _nsa_sample_layer(x, g, w, cache, layer, win_buf, page_table, tabs, bd):
    b, d = x.shape
    n_pages = page_table.shape[1]
    past = n_pages * PAGE_SIZE
    qn, qr, gates, rows, wrows, _, _, _, _ = _nsa_proj(
        x, g, w["w_in"], w["q_gain"], w["k_gain"], tabs[0], tabs[1], bd, b)
    pool = cache.shape[1]
    cache4 = cache.reshape(cache.shape[0], pool, PAGE_SIZE, 4 * KV_W)
    kc, vc = _compress_sample(cache4, layer, page_table, w["pex"], w["w1p"], w["w2p"], w["k_gain"], bd)
    n_seg = past // CMP_STRIDE
    n_cmp = (past + 1 - CMP_LEN) // CMP_STRIDE + 1
    n_slc = -(-(past + 1) // SEL_BLOCK)
    n_past_blk = past // SEL_BLOCK
    nsp = -(-n_slc // LANES) * LANES
    msel = _cmp_to_sel(n_seg, n_cmp, nsp, n_slc)
    ocmp, idx = _sample_cmp(qn.reshape(b, 1, d), kc, vc, msel, n_cmp, n_slc)
    wb = win_buf.shape[1]
    win_new = jnp.concatenate([win_buf.reshape(b, wb, 2 * KV_W)[:, 1:], wrows.reshape(b, 1, 2 * KV_W)], axis=1)
    cache_blk = cache.reshape(cache.shape[0], pool * (PAGE_SIZE // SEL_BLOCK), SEL_BLOCK, 4 * KV_W)
    o4 = _sample_attn(idx[:, :N_KV_HEADS, :N_SEL], page_table, cache_blk, layer, qr.reshape(b, 1, d), rows.reshape(b, 1, -1),
                      win_new, gates.reshape(b, 1, LANES), ocmp, n_past_blk)
    o = jnp.stack([o4[:, gi, :, (gi % 2) * HEAD_DIM:(gi % 2 + 1) * HEAD_DIM] for gi in range(N_KV_HEADS)], axis=1)
    x = _linear_res(o.reshape(b, d), w["w_out"], jnp.zeros((d,), F32), x)
    return x, rows.reshape(b, 1, 4, N_KV_HEADS, HEAD_DIM), win_new.reshape(b, wb, 2, N_KV_HEADS, HEAD_DIM)


def _rwkv_weights(mix, w_rkv, w0, w1, w2, a0, a1, a2, g1, g2, k_k, k_a, r_k, ln_w, ln_b, w_out):
    d = w0.shape[0]
    row = lambda a: a.reshape(1, d)
    return dict(mix=jnp.pad(mix, ((0, 2), (0, 0))), w_rkv=w_rkv.astype(BF16), w0=row(w0), w1=w1.astype(BF16),
                w2=w2.astype(BF16), a0=row(a0), a1=a1.astype(BF16), a2=a2.astype(BF16), g1=g1.astype(BF16),
                g2=g2.astype(BF16), k_k=row(k_k), k_a=row(k_a), r_k=row(r_k), ln_w=row(ln_w), ln_b=row(ln_b),
                w_out=w_out.astype(BF16))


def _pair_states_to_heads(sp):
    b, n_pairs = sp.shape[:2]
    n = RWKV_HEAD
    return jnp.stack([sp[:, :, :n, :n], sp[:, :, n:, n:]], axis=2).reshape(b, 2 * n_pairs, n, n)


def _rwkv_prompt_layer(x, g, w, b, t_len, bd):
    m, d = x.shape
    r, lw, k, v, kk, bb, gate, h_last = _rwkv_proj(x, None, jnp.zeros((b, 1, d), F32), t_len, g, w, bd)
    y, sp = _rwkv_scan(r, lw, k, v, kk, bb, b, t_len)
    out = _rwkv_post(y, r, k, v, gate, x, w["ln_w"], w["ln_b"], w["r_k"], bd, w["w_out"])
    return out, sp, h_last[:, 7]


def _rwkv_sample_layer(x, g, w, shift, state, bd):
    b, d = x.shape
    r, lw, k, v, kk, bb, gate, h = _rwkv_proj(x, shift, None, b, g, w, bd)
    y, s_new = _rwkv_step(state, r, lw, k, kk, bb, v)
    out = _rwkv_post(y, r, k, v, gate, x, w["ln_w"], w["ln_b"], w["r_k"], bd, w["w_out"])
    return out, s_new, h


def kernel(x_prompt, x_sample, cache_nsa, cache_nsa_win, state_rwkv_wkv, state_rwkv_shift, state_conv, page_table, norms, ffn_w_gu, ffn_w_down, nsa_w_in, nsa_q_norm, nsa_k_norm, nsa_cmp_pe, nsa_cmp_w1, nsa_cmp_w2, nsa_w_out, rwkv_mix, rwkv_w_rkv, rwkv_w0, rwkv_w1, rwkv_w2, rwkv_a0, rwkv_a1, rwkv_a2, rwkv_g1, rwkv_g2, rwkv_k_k, rwkv_k_a, rwkv_r_k, rwkv_ln_w, rwkv_ln_b, rwkv_w_out, conv_w_pw1, conv_b_pw1, conv_w_dw, conv_b_dw, conv_ln_w, conv_ln_b, conv_w_pw2, conv_b_pw2):
    bp, t_len, d = x_prompt.shape
    bs = x_sample.shape[0]
    depth = norms.shape[0]
    past = page_table.shape[1] * PAGE_SIZE
    xp = x_prompt.reshape(bp * t_len, d)
    xs = x_sample.reshape(bs, d)
    bd = _block_ones()
    tabs_p = _rope_tables(jnp.arange(t_len))
    tabs_s = _rope_tables(jnp.full((bs,), past, jnp.int32))
    rows_p, rows_s, win_p, win_s = [], [], [], []
    wkv_p, wkv_s, shift_p, shift_s, conv_p, conv_s = [], [], [], [], [], []
    for i in range(depth):
        kind, j = i % 3, i // 3
        g0 = norms[i, 0].reshape(1, d)
        if kind == 0:
            w = _nsa_weights(nsa_w_in[j], nsa_q_norm[j], nsa_k_norm[j], nsa_cmp_pe[j], nsa_cmp_w1[j],
                             nsa_cmp_w2[j], nsa_w_out[j], d)
            xp, r_p, wn_p = _nsa_prompt_layer(xp, g0, w, bp, t_len, tabs_p, bd)
            xs, r_s, wn_s = _nsa_sample_layer(xs, g0, w, cache_nsa, j, cache_nsa_win[j], page_table, tabs_s, bd)
            rows_p.append(r_p)
            rows_s.append(r_s)
            win_p.append(wn_p)
            win_s.append(wn_s)
        elif kind == 1:
            w = _rwkv_weights(rwkv_mix[j], rwkv_w_rkv[j], rwkv_w0[j], rwkv_w1[j], rwkv_w2[j], rwkv_a0[j],
                              rwkv_a1[j], rwkv_a2[j], rwkv_g1[j], rwkv_g2[j], rwkv_k_k[j], rwkv_k_a[j],
                              rwkv_r_k[j].reshape(-1), rwkv_ln_w[j], rwkv_ln_b[j], rwkv_w_out[j])
            xp, sp, sh_p = _rwkv_prompt_layer(xp, g0, w, bp, t_len, bd)
            xs, ss, sh_s = _rwkv_sample_layer(xs, g0, w, state_rwkv_shift[j], state_rwkv_wkv[j], bd)
            shift_p.append(sh_p)
            shift_s.append(sh_s)
            wkv_p.append(_pair_states_to_heads(sp))
            wkv_s.append(ss)
        else:
            row = lambda a: a.reshape(1, -1)
            w = dict(w_pw1=conv_w_pw1[j].astype(BF16), b_pw1=row(conv_b_pw1[j]), w_dw=jnp.pad(conv_w_dw[j], ((0, 1), (0, 0))),
                     b_dw=row(conv_b_dw[j]), ln_w=row(conv_ln_w[j]), ln_b=row(conv_ln_b[j]),
                     w_pw2=conv_w_pw2[j].astype(BF16), b_pw2=row(conv_b_pw2[j]))
            xp, tail = _conv_prompt(xp, jnp.zeros((bp, CONV_HALO, d), F32), bp, t_len, g0, w)
            conv_p.append(tail[:, CONV_HALO - (CONV_W - 1):])
            buf = state_conv[j]
            xs, glu_s = _conv_sample(xs, jnp.swapaxes(buf, 0, 1), g0, w)
            conv_s.append(jnp.concatenate([buf[:, 1:], glu_s[:, None]], axis=1))
        g1 = norms[i, 1]
        w_gu = ffn_w_gu[i].astype(BF16)
        w_dn = ffn_w_down[i].astype(BF16)
        xp = _ffn(xp, g1, w_gu, w_dn)
        xs = _ffn(xs, g1, w_gu, w_dn)
    return (xp.reshape(bp, t_len, d), xs.reshape(bs, 1, d), jnp.stack(rows_p), jnp.stack(rows_s),
            jnp.stack(win_p), jnp.stack(win_s), jnp.stack(wkv_p), jnp.stack(wkv_s),
            jnp.stack(shift_p), jnp.stack(shift_s), jnp.stack(conv_p), jnp.stack(conv_s))
```
